```python
import jax, jax.numpy as jnp
from jax import lax
import numpy as np

D_MODEL = 1024
BATCH = 2
SEQ = 8192
DEPTH = 2
DEC_BATCH = 16
DEC_SEQ = 2048
PAST_LEN = 128

POOL_WIDTH = D_MODEL
N_POOL_GROUPS = 4
POOL_GROUP = POOL_WIDTH // N_POOL_GROUPS
POOL_WINDOWS = (2, 4, 8, 16)
CONV_WIDTH = D_MODEL
CONV_K = 3
IN_COLS = POOL_WIDTH + 3 * CONV_WIDTH + 2 * D_MODEL
FFN_HIDDEN = ((8 * D_MODEL // 3 + 255) // 256) * 256
EPS = 1e-6

kernel_name = "hybrid_pool_shortconv_encoder"


def rmsnorm(x, g):
    xf = x.astype(jnp.float32)
    r = xf * lax.rsqrt(jnp.mean(xf * xf, axis=-1, keepdims=True) + EPS)
    return (r * g.astype(jnp.float32)).astype(x.dtype)


def centred_mean_minus_self(g, w):
    s = g.shape[1]
    half = w // 2
    gp = jnp.pad(g.astype(jnp.float32), ((0, 0), (half, half), (0, 0)))
    cs = jnp.cumsum(gp, axis=1)
    cs = jnp.concatenate([jnp.zeros_like(cs[:, :1]), cs], axis=1)
    wsum = cs[:, w:w + s] - cs[:, :s]
    pos = jnp.arange(s, dtype=jnp.int32)
    cnt = (jnp.minimum(pos + half, s) - jnp.maximum(pos - half, 0)).astype(jnp.float32)
    mean = wsum / cnt[None, :, None]
    return (mean - g.astype(jnp.float32)).astype(g.dtype)


def multiscale_pool_branch(u_pool, pool_w, pool_scale):
    outs = []
    for i, w in enumerate(POOL_WINDOWS):
        g = u_pool[..., i * POOL_GROUP:(i + 1) * POOL_GROUP]
        d = centred_mean_minus_self(g, w)
        outs.append(jnp.einsum('bsg,gh->bsh', d, pool_w[i]))
    return jnp.concatenate(outs, axis=-1) * pool_scale


def short_conv_branch(b_gate, c_gate, v, conv_w):
    cv = c_gate * v
    s = cv.shape[1]
    cp = jnp.pad(cv, ((0, 0), (1, 1), (0, 0)))
    y = cp[:, 0:s] * conv_w[0] + cp[:, 1:s + 1] * conv_w[1] + cp[:, 2:s + 2] * conv_w[2]
    return b_gate * y


def encoder_layer(x, w_in, pool_w, pool_scale, conv_w, w_out, w_ffn_gate, w_ffn_up, w_ffn_down,
                  g_pre_mix, g_post_mix, g_pre_ffn, g_post_ffn):
    h = rmsnorm(x, g_pre_mix)
    u = jnp.einsum('bsd,dc->bsc', h, w_in)
    o = 0
    u_pool = u[..., o:o + POOL_WIDTH]; o += POOL_WIDTH
    b_gate = u[..., o:o + CONV_WIDTH]; o += CONV_WIDTH
    c_gate = u[..., o:o + CONV_WIDTH]; o += CONV_WIDTH
    v = u[..., o:o + CONV_WIDTH]; o += CONV_WIDTH
    ga = u[..., o:o + D_MODEL]; o += D_MODEL
    gb = u[..., o:o + D_MODEL]
    y_a = multiscale_pool_branch(u_pool, pool_w, pool_scale)
    y_b = short_conv_branch(b_gate, c_gate, v, conv_w)
    m = jax.nn.sigmoid(ga) * y_a + jax.nn.sigmoid(gb) * y_b
    x = x + rmsnorm(jnp.einsum('bsd,de->bse', m, w_out), g_post_mix)
    f = rmsnorm(x, g_pre_ffn)
    hid = jax.nn.silu(jnp.einsum('bsd,df->bsf', f, w_ffn_gate)) * jnp.einsum('bsd,df->bsf', f, w_ffn_up)
    x = x + rmsnorm(jnp.einsum('bsf,fd->bsd', hid, w_ffn_down), g_post_ffn)
    return x


def setup_inputs(seed: int = 0) -> dict:
    key = jax.random.key(seed)
    ks = jax.random.split(key, 16)
    f32 = jnp.float32
    x_prompt = jax.random.normal(ks[0], (BATCH, SEQ, D_MODEL), f32)
    x_sample = jax.random.normal(ks[1], (DEC_BATCH, DEC_SEQ, D_MODEL), f32)
    w_in = jax.random.normal(ks[2], (DEPTH, D_MODEL, IN_COLS), f32) * D_MODEL ** -0.5
    pool_w = jax.random.normal(ks[3], (DEPTH, N_POOL_GROUPS, POOL_GROUP, POOL_GROUP), f32) * POOL_GROUP ** -0.5
    pool_scale = 1.0 + 0.1 * jax.random.normal(ks[4], (DEPTH, POOL_WIDTH), f32)
    conv_w = jax.random.normal(ks[5], (DEPTH, CONV_K, CONV_WIDTH), f32) * CONV_K ** -0.5
    w_out = jax.random.normal(ks[6], (DEPTH, D_MODEL, D_MODEL), f32) * D_MODEL ** -0.5
    w_ffn_gate = jax.random.normal(ks[7], (DEPTH, D_MODEL, FFN_HIDDEN), f32) * D_MODEL ** -0.5
    w_ffn_up = jax.random.normal(ks[8], (DEPTH, D_MODEL, FFN_HIDDEN), f32) * D_MODEL ** -0.5
    w_ffn_down = jax.random.normal(ks[9], (DEPTH, FFN_HIDDEN, D_MODEL), f32) * FFN_HIDDEN ** -0.5
    g_pre_mix = 1.0 + 0.05 * jax.random.normal(ks[10], (DEPTH, D_MODEL), f32)
    g_post_mix = 1.0 + 0.05 * jax.random.normal(ks[11], (DEPTH, D_MODEL), f32)
    g_pre_ffn = 1.0 + 0.05 * jax.random.normal(ks[12], (DEPTH, D_MODEL), f32)
    g_post_ffn = 1.0 + 0.05 * jax.random.normal(ks[13], (DEPTH, D_MODEL), f32)
    return {"x_prompt": x_prompt, "x_sample": x_sample, "w_in": w_in, "pool_w": pool_w,
            "pool_scale": pool_scale, "conv_w": conv_w, "w_out": w_out, "w_ffn_gate": w_ffn_gate,
            "w_ffn_up": w_ffn_up, "w_ffn_down": w_ffn_down, "g_pre_mix": g_pre_mix,
            "g_post_mix": g_post_mix, "g_pre_ffn": g_pre_ffn, "g_post_ffn": g_post_ffn}


def reference(x_prompt, x_sample, w_in, pool_w, pool_scale, conv_w, w_out, w_ffn_gate, w_ffn_up,
              w_ffn_down, g_pre_mix, g_post_mix, g_pre_ffn, g_post_ffn):
    y_prompt = x_prompt
    y_sample = x_sample
    for l in range(DEPTH):
        layer_params = (w_in[l], pool_w[l], pool_scale[l], conv_w[l], w_out[l], w_ffn_gate[l],
                        w_ffn_up[l], w_ffn_down[l], g_pre_mix[l], g_post_mix[l], g_pre_ffn[l],
                        g_post_ffn[l])
        y_prompt = encoder_layer(y_prompt, *layer_params)
        y_sample = encoder_layer(y_sample, *layer_params)
    return (y_prompt, y_sample)
```

```python
import functools

import jax
import jax.numpy as jnp
from jax import lax
from jax.experimental import pallas as pl
from jax.experimental.pallas import tpu as pltpu

D_MODEL = 1024
POOL_WINDOWS = (2, 4, 8, 16)
POOL_GROUP = D_MODEL // len(POOL_WINDOWS)
FFN_HIDDEN = 2816
EPS = 1e-6

MXU_TILE_V7X = 256
BF16_SUBLANES_V7X = 16
VMEM_LIMIT_BYTES_V7X = 56 * 1024 * 1024

ROW_TILE = 512
HALO = BF16_SUBLANES_V7X
COL_CHUNK = MXU_TILE_V7X

assert max(POOL_WINDOWS) // 2 <= HALO
assert POOL_GROUP == COL_CHUNK


def _rmsnorm(x, g):
    return x * lax.rsqrt(jnp.mean(x * x, axis=-1, keepdims=True) + EPS) * g


def _dot(a, b):
    return jnp.dot(a, b, preferred_element_type=jnp.float32)


def _mixer_kernel(xp_ref, x_ref, xn_ref, w_in_ref, pool_w_ref, pool_scale_ref, conv_w_ref,
                  w_out_ref, g_pre_ref, g_post_ref, o_ref, h_ref, m_ref, *, seq_len):
    t = x_ref.shape[0]
    pos0 = (pl.program_id(0) * t) & (seq_len - 1)
    g_pre = g_pre_ref[...]

    has_prev = pos0 != 0
    has_next = pos0 + t != seq_len
    h_ref[0:HALO, :] = jnp.where(has_prev, _rmsnorm(xp_ref[...], g_pre), 0.0).astype(jnp.bfloat16)
    h_ref[HALO:HALO + t, :] = _rmsnorm(x_ref[...], g_pre).astype(jnp.bfloat16)
    h_ref[HALO + t:, :] = jnp.where(has_next, _rmsnorm(xn_ref[...], g_pre), 0.0).astype(jnp.bfloat16)

    h_all = h_ref[...]
    h_mid = h_ref[HALO:HALO + t, :]
    pos = pos0 + lax.broadcasted_iota(jnp.int32, (t, COL_CHUNK), 0)

    def w_in_cols(section, c):
        off = section * D_MODEL + c * COL_CHUNK
        return w_in_ref[:, off:off + COL_CHUNK]

    for c, w in enumerate(POOL_WINDOWS):
        half = w // 2
        lo, hi = c * COL_CHUNK, (c + 1) * COL_CHUNK
        up = _dot(h_all, w_in_cols(0, c))
        a = up[HALO - half:HALO - half + t + w - 1]
        k = 1
        while k < w:
            n = a.shape[0]
            a = a[:n - k] + a[k:]
            k *= 2
        cnt = jnp.minimum(pos + half, seq_len) - jnp.maximum(pos - half, 0)
        d = a / cnt.astype(jnp.float32) - up[HALO:HALO + t]
        y_a = _dot(d.astype(jnp.bfloat16), pool_w_ref[c]) * pool_scale_ref[:, lo:hi]
        cv = _dot(h_all, w_in_cols(2, c)) * _dot(h_all, w_in_cols(3, c))
        conv_w = conv_w_ref[:, lo:hi]
        y_b = (cv[HALO - 1:HALO - 1 + t] * conv_w[0:1]
               + cv[HALO:HALO + t] * conv_w[1:2]
               + cv[HALO + 1:HALO + 1 + t] * conv_w[2:3])
        y_b = _dot(h_mid, w_in_cols(1, c)) * y_b
        ga = _dot(h_mid, w_in_cols(4, c))
        gb = _dot(h_mid, w_in_cols(5, c))
        m = jax.nn.sigmoid(ga) * y_a + jax.nn.sigmoid(gb) * y_b
        m_ref[:, lo:hi] = m.astype(jnp.bfloat16)

    r = _dot(m_ref[...], w_out_ref[...])
    o_ref[...] = x_ref[...] + _rmsnorm(r, g_post_ref[...])


def _ffn_kernel(x_ref, w_gate_ref, w_up_ref, w_down_ref, g_pre_ref, g_post_ref, o_ref, hid_ref):
    x = x_ref[...]
    f = _rmsnorm(x, g_pre_ref[...]).astype(jnp.bfloat16)
    for c in range(FFN_HIDDEN // COL_CHUNK):
        lo, hi = c * COL_CHUNK, (c + 1) * COL_CHUNK
        gate = _dot(f, w_gate_ref[:, lo:hi])
        up = _dot(f, w_up_ref[:, lo:hi])
        hid_ref[:, lo:hi] = (jax.nn.silu(gate) * up).astype(jnp.bfloat16)
    r = _dot(hid_ref[...], w_down_ref[...])
    o_ref[...] = x + _rmsnorm(r, g_post_ref[...])


def _resident(shape):
    return pl.BlockSpec(shape, lambda i: (0,) * len(shape), pipeline_mode=pl.Buffered(1))


def _mixer(x, seq_len, w_in, pool_w, pool_scale, conv_w, w_out, g_pre, g_post):
    rows = x.shape[0]
    t = ROW_TILE
    halo_per_tile = t // HALO
    n_halo_blocks = rows // HALO
    row_spec = pl.BlockSpec((t, D_MODEL), lambda i: (i, 0))
    prev_spec = pl.BlockSpec((HALO, D_MODEL), lambda i: (jnp.maximum(i * halo_per_tile - 1, 0), 0))
    next_spec = pl.BlockSpec(
        (HALO, D_MODEL), lambda i: (jnp.minimum((i + 1) * halo_per_tile, n_halo_blocks - 1), 0))
    return pl.pallas_call(
        functools.partial(_mixer_kernel, seq_len=seq_len),
        grid=(rows // t,),
        in_specs=[prev_spec, row_spec, next_spec,
                  _resident(w_in.shape), _resident(pool_w.shape), _resident(pool_scale.shape),
                  _resident(conv_w.shape), _resident(w_out.shape),
                  _resident(g_pre.shape), _resident(g_post.shape)],
        out_specs=row_spec,
        out_shape=jax.ShapeDtypeStruct(x.shape, x.dtype),
        scratch_shapes=[pltpu.VMEM((t + 2 * HALO, D_MODEL), jnp.bfloat16),
                        pltpu.VMEM((t, D_MODEL), jnp.bfloat16)],
        compiler_params=pltpu.CompilerParams(
            dimension_semantics=("arbitrary",), vmem_limit_bytes=VMEM_LIMIT_BYTES_V7X),
    )(x, x, x, w_in, pool_w, pool_scale, conv_w, w_out, g_pre, g_post)


def _ffn(x, w_gate, w_up, w_down, g_pre, g_post):
    rows = x.shape[0]
    t = ROW_TILE
    row_spec = pl.BlockSpec((t, D_MODEL), lambda i: (i, 0))
    return pl.pallas_call(
        _ffn_kernel,
        grid=(rows // t,),
        in_specs=[row_spec, _resident(w_gate.shape), _resident(w_up.shape),
                  _resident(w_down.shape), _resident(g_pre.shape), _resident(g_post.shape)],
        out_specs=row_spec,
        out_shape=jax.ShapeDtypeStruct(x.shape, x.dtype),
        scratch_shapes=[pltpu.VMEM((t, FFN_HIDDEN), jnp.bfloat16)],
        compiler_params=pltpu.CompilerParams(
            dimension_semantics=("arbitrary",), vmem_limit_bytes=VMEM_LIMIT_BYTES_V7X),
    )(x, w_gate, w_up, w_down, g_pre, g_post)


def kernel(x_prompt, x_sample, w_in, pool_w, pool_scale, conv_w, w_out, w_ffn_gate, w_ffn_up,
           w_ffn_down, g_pre_mix, g_post_mix, g_pre_ffn, g_post_ffn):
    depth = w_in.shape[0]
    bf16 = jnp.bfloat16
    w_in, pool_w, w_out = w_in.astype(bf16), pool_w.astype(bf16), w_out.astype(bf16)
    w_ffn_gate, w_ffn_up, w_ffn_down = (w_ffn_gate.astype(bf16), w_ffn_up.astype(bf16),
                                        w_ffn_down.astype(bf16))
    outs = []
    for x in (x_prompt, x_sample):
        batch, seq_len, _ = x.shape
        assert seq_len & (seq_len - 1) == 0 and seq_len % ROW_TILE == 0
        y = x.reshape(batch * seq_len, D_MODEL)
        for l in range(depth):
            y = _mixer(y, seq_len, w_in[l], pool_w[l], pool_scale[l][None], conv_w[l], w_out[l],
                       g_pre_mix[l][None], g_post_mix[l][None])
            y = _ffn(y, w_ffn_gate[l], w_ffn_up[l], w_ffn_down[l],
                     g_pre_ffn[l][None], g_post_ffn[l][None])
        outs.append(y.reshape(x.shape))
    return tuple(outs)
```

```python
import functools

import jax
import jax.numpy as jnp
from jax import lax
from jax.experimental import pallas as pl
from jax.experimental.pallas import tpu as pltpu

D_MODEL = 1024
POOL_WINDOWS = (2, 4, 8, 16)
POOL_GROUP = D_MODEL // len(POOL_WINDOWS)
FFN_HIDDEN = 2816
EPS = 1e-6

MXU_TILE_V7X = 256
BF16_SUBLANES_V7X = 16
VMEM_LIMIT_BYTES_V7X = 56 * 1024 * 1024

ROW_TILE = 512
HALO = BF16_SUBLANES_V7X
COL_CHUNK = MXU_TILE_V7X

assert max(POOL_WINDOWS) // 2 <= HALO
assert POOL_GROUP == COL_CHUNK


def _rmsnorm(x, g):
    return x * lax.rsqrt(jnp.mean(x * x, axis=-1, keepdims=True) + EPS) * g


def _dot(a, b):
    return jnp.dot(a, b, preferred_element_type=jnp.float32)


def _mix_chunk(h_ref, m_ref, pos, c, seq_len, w_in_ref, pool_w_ref, pool_scale_ref, conv_w_ref):
    t = m_ref.shape[0]
    w = POOL_WINDOWS[c]
    half = w // 2
    lo, hi = c * COL_CHUNK, (c + 1) * COL_CHUNK

    def w_in_cols(section):
        off = section * D_MODEL + lo
        return w_in_ref[:, off:off + COL_CHUNK]

    h_all = h_ref[...]
    h_mid = h_ref[HALO:HALO + t, :]
    up = _dot(h_all, w_in_cols(0))
    a = up[HALO - half:HALO - half + t + w - 1]
    k = 1
    while k < w:
        n = a.shape[0]
        a = a[:n - k] + a[k:]
        k *= 2
    cnt = jnp.minimum(pos + half, seq_len) - jnp.maximum(pos - half, 0)
    d = a / cnt.astype(jnp.float32) - up[HALO:HALO + t]
    y_a = _dot(d.astype(jnp.bfloat16), pool_w_ref[c]) * pool_scale_ref[:, lo:hi]
    cv = _dot(h_all, w_in_cols(2)) * _dot(h_all, w_in_cols(3))
    conv_w = conv_w_ref[:, lo:hi]
    y_b = (cv[HALO - 1:HALO - 1 + t] * conv_w[0:1]
           + cv[HALO:HALO + t] * conv_w[1:2]
           + cv[HALO + 1:HALO + 1 + t] * conv_w[2:3])
    y_b = _dot(h_mid, w_in_cols(1)) * y_b
    ga = _dot(h_mid, w_in_cols(4))
    gb = _dot(h_mid, w_in_cols(5))
    m = jax.nn.sigmoid(ga) * y_a + jax.nn.sigmoid(gb) * y_b
    m_ref[:, lo:hi] = m.astype(jnp.bfloat16)


def _mixer_kernel(x_ref, xan_ref, xnn_ref, w_in_ref, pool_w_ref, pool_scale_ref, conv_w_ref,
                  w_out_ref, g_pre_ref, g_post_ref, o_ref, ha_ref, hb_ref, ma_ref, mb_ref, *,
                  seq_len):
    t = ROW_TILE
    step = pl.program_id(0)
    pos_a0 = (step * 2 * t) & (seq_len - 1)
    g_pre = g_pre_ref[...]

    def norm(x):
        return _rmsnorm(x, g_pre).astype(jnp.bfloat16)

    @pl.when(step == 0)
    def _():
        ha_ref[0:HALO, :] = jnp.zeros((HALO, D_MODEL), jnp.bfloat16)
        ha_ref[HALO:HALO + t, :] = norm(x_ref[0:t, :])
        ha_ref[HALO + t:, :] = norm(x_ref[t:t + HALO, :])

    b_has_next = pos_a0 + 2 * t != seq_len
    hb_ref[0:HALO, :] = norm(x_ref[t - HALO:t, :])
    hb_ref[HALO:HALO + t, :] = norm(x_ref[t:2 * t, :])
    hb_ref[HALO + t:, :] = jnp.where(b_has_next, norm(xan_ref[0:HALO, :]), 0.0).astype(jnp.bfloat16)

    pos_a = pos_a0 + lax.broadcasted_iota(jnp.int32, (t, COL_CHUNK), 0)
    pos_b = pos_a + t
    weights = (w_in_ref, pool_w_ref, pool_scale_ref, conv_w_ref)
    n_chunks = len(POOL_WINDOWS)
    for c in range(n_chunks):
        _mix_chunk(ha_ref, ma_ref, pos_a, c, seq_len, *weights)
        if c == n_chunks - 1:
            an_has_prev = ((pos_a0 + 2 * t) & (seq_len - 1)) != 0
            ha_ref[0:HALO, :] = jnp.where(
                an_has_prev, norm(x_ref[2 * t - HALO:2 * t, :]), 0.0).astype(jnp.bfloat16)
            ha_ref[HALO:HALO + t, :] = norm(xan_ref[...])
            ha_ref[HALO + t:, :] = norm(xnn_ref[...])
        _mix_chunk(hb_ref, mb_ref, pos_b, c, seq_len, *weights)

    g_post = g_post_ref[...]
    r_a = _dot(ma_ref[...], w_out_ref[...])
    r_b = _dot(mb_ref[...], w_out_ref[...])
    o_ref[0:t, :] = x_ref[0:t, :] + _rmsnorm(r_a, g_post)
    o_ref[t:2 * t, :] = x_ref[t:2 * t, :] + _rmsnorm(r_b, g_post)


def _ffn_kernel(x_ref, w_gate_ref, w_up_ref, w_down_ref, g_pre_ref, g_post_ref, o_ref,
                f_ref, hid_ref):
    t = ROW_TILE
    halves = (slice(0, t), slice(t, 2 * t))
    g_pre = g_pre_ref[...]
    for rows in halves:
        f_ref[rows, :] = _rmsnorm(x_ref[rows, :], g_pre).astype(jnp.bfloat16)
    for c in range(FFN_HIDDEN // COL_CHUNK):
        cols = slice(c * COL_CHUNK, (c + 1) * COL_CHUNK)
        for rows in halves:
            f = f_ref[rows, :]
            gate = _dot(f, w_gate_ref[:, cols])
            up = _dot(f, w_up_ref[:, cols])
            hid_ref[rows, cols] = (jax.nn.silu(gate) * up).astype(jnp.bfloat16)
    g_post = g_post_ref[...]
    for rows in halves:
        r = _dot(hid_ref[rows, :], w_down_ref[...])
        o_ref[rows, :] = x_ref[rows, :] + _rmsnorm(r, g_post)


def _layer_resident(stacked, layer):
    shape = stacked.shape[1:]
    return pl.BlockSpec((None,) + shape, lambda i: (layer,) + (0,) * len(shape),
                        pipeline_mode=pl.Buffered(1))


def _compiler_params():
    return pltpu.CompilerParams(dimension_semantics=("arbitrary",),
                                vmem_limit_bytes=VMEM_LIMIT_BYTES_V7X)


def _mixer(x, seq_len, layer, w_in, pool_w, pool_scale, conv_w, w_out, g_pre, g_post):
    rows = x.shape[0]
    t = ROW_TILE
    n_tiles = rows // t
    n_halo_blocks = rows // HALO
    step_spec = pl.BlockSpec((2 * t, D_MODEL), lambda i: (i, 0))
    next_a_spec = pl.BlockSpec((t, D_MODEL), lambda i: (jnp.minimum(2 * i + 2, n_tiles - 1), 0))
    next_halo_spec = pl.BlockSpec(
        (HALO, D_MODEL),
        lambda i: (jnp.minimum((2 * i + 3) * (t // HALO), n_halo_blocks - 1), 0))
    params = (w_in, pool_w, pool_scale, conv_w, w_out, g_pre, g_post)
    return pl.pallas_call(
        functools.partial(_mixer_kernel, seq_len=seq_len),
        grid=(rows // (2 * t),),
        in_specs=[step_spec, next_a_spec, next_halo_spec] + [_layer_resident(p, layer) for p in params],
        out_specs=step_spec,
        out_shape=jax.ShapeDtypeStruct(x.shape, x.dtype),
        scratch_shapes=[pltpu.VMEM((t + 2 * HALO, D_MODEL), jnp.bfloat16),
                        pltpu.VMEM((t + 2 * HALO, D_MODEL), jnp.bfloat16),
                        pltpu.VMEM((t, D_MODEL), jnp.bfloat16),
                        pltpu.VMEM((t, D_MODEL), jnp.bfloat16)],
        compiler_params=_compiler_params(),
        name="mixer",
    )(x, x, x, *params)


def _ffn(x, layer, w_gate, w_up, w_down, g_pre, g_post):
    rows = x.shape[0]
    t = ROW_TILE
    step_spec = pl.BlockSpec((2 * t, D_MODEL), lambda i: (i, 0))
    params = (w_gate, w_up, w_down, g_pre, g_post)
    return pl.pallas_call(
        _ffn_kernel,
        grid=(rows // (2 * t),),
        in_specs=[step_spec] + [_layer_resident(p, layer) for p in params],
        out_specs=step_spec,
        out_shape=jax.ShapeDtypeStruct(x.shape, x.dtype),
        scratch_shapes=[pltpu.VMEM((2 * t, D_MODEL), jnp.bfloat16),
                        pltpu.VMEM((2 * t, FFN_HIDDEN), jnp.bfloat16)],
        compiler_params=_compiler_params(),
        name="ffn",
    )(x, *params)


def kernel(x_prompt, x_sample, w_in, pool_w, pool_scale, conv_w, w_out, w_ffn_gate, w_ffn_up,
           w_ffn_down, g_pre_mix, g_post_mix, g_pre_ffn, g_post_ffn):
    depth = w_in.shape[0]
    bf16 = jnp.bfloat16
    w_in, pool_w, w_out = w_in.astype(bf16), pool_w.astype(bf16), w_out.astype(bf16)
    w_ffn_gate, w_ffn_up, w_ffn_down = (w_ffn_gate.astype(bf16), w_ffn_up.astype(bf16),
                                        w_ffn_down.astype(bf16))
    pool_scale, g_pre_mix, g_post_mix, g_pre_ffn, g_post_ffn = (
        p[:, None, :] for p in (pool_scale, g_pre_mix, g_post_mix, g_pre_ffn, g_post_ffn))
    outs = []
    for x in (x_prompt, x_sample):
        batch, seq_len, _ = x.shape
        assert seq_len & (seq_len - 1) == 0 and seq_len % (2 * ROW_TILE) == 0
        y = x.reshape(batch * seq_len, D_MODEL)
        for layer in range(depth):
            y = _mixer(y, seq_len, layer, w_in, pool_w, pool_scale, conv_w, w_out,
                       g_pre_mix, g_post_mix)
            y = _ffn(y, layer, w_ffn_gate, w_ffn_up, w_ffn_down, g_pre_ffn, g_post_ffn)
        outs.append(y.reshape(x.shape))
    return tuple(outs)
```

```python
import functools

import jax
import jax.numpy as jnp
from jax import lax
from jax.experimental import pallas as pl
from jax.experimental.pallas import tpu as pltpu

D_MODEL = 1024
POOL_WINDOWS = (2, 4, 8, 16)
POOL_GROUP = D_MODEL // len(POOL_WINDOWS)
FFN_HIDDEN = 2816
EPS = 1e-6

MXU_TILE_V7X = 256
BF16_SUBLANES_V7X = 16
SUBLANES_V7X = 8
LANES_V7X = 128
VMEM_LIMIT_BYTES_V7X = 56 * 1024 * 1024

ROW_TILE = 512
HALO = BF16_SUBLANES_V7X
COL_CHUNK = MXU_TILE_V7X

assert max(POOL_WINDOWS) // 2 <= HALO
assert POOL_GROUP == COL_CHUNK


def _rmsnorm(x, g):
    return x * lax.rsqrt(jnp.mean(x * x, axis=-1, keepdims=True) + EPS) * g


def _dot(a, b):
    return jnp.dot(a, b, preferred_element_type=jnp.float32)


def _windowed_sum(st_ref, j, w, t):
    base = HALO - w // 2
    q = 1 if w <= 4 else w // 4

    def rows_sum(slot, start, n_terms, stride, length):
        acc = st_ref[slot, j, pl.ds(start, length), :]
        for k in range(1, n_terms):
            acc = acc + st_ref[slot, j, pl.ds(start + k * stride, length), :]
        return acc

    if q == 1:
        return rows_sum(0, base, w, 1, t)
    part_len = t + w - q
    part_len += -part_len % SUBLANES_V7X
    st_ref[2, j, 0:part_len, :] = rows_sum(0, base, q, 1, part_len)
    return rows_sum(2, 0, w // q, q, t)


def _mix_chunk(h_ref, m_ref, st_ref, pos, c, seq_len, w_in_ref, pool_w_ref, pool_scale_ref,
               conv_w_ref):
    t = m_ref.shape[0]
    w = POOL_WINDOWS[c]
    half = w // 2
    lo, hi = c * COL_CHUNK, (c + 1) * COL_CHUNK
    n_slabs = COL_CHUNK // LANES_V7X

    def w_in_cols(section):
        off = section * D_MODEL + lo
        return w_in_ref[:, off:off + COL_CHUNK]

    def slabs(x):
        return [x[:, j * LANES_V7X:(j + 1) * LANES_V7X] for j in range(n_slabs)]

    h_all = h_ref[...]
    h_mid = h_ref[HALO:HALO + t, :]
    for j, up in enumerate(slabs(_dot(h_all, w_in_cols(0)))):
        st_ref[0, j] = up
    cnt = jnp.minimum(pos + half, seq_len) - jnp.maximum(pos - half, 0)
    inv_cnt = 1.0 / cnt.astype(jnp.float32)
    d = jnp.concatenate(
        [_windowed_sum(st_ref, j, w, t) * inv_cnt - st_ref[0, j, HALO:HALO + t, :]
         for j in range(n_slabs)], axis=1)
    y_a = _dot(d.astype(jnp.bfloat16), pool_w_ref[c]) * pool_scale_ref[:, lo:hi]
    for j, cv in enumerate(slabs(_dot(h_all, w_in_cols(2)) * _dot(h_all, w_in_cols(3)))):
        st_ref[1, j] = cv
    conv_w = slabs(conv_w_ref[:, lo:hi])
    conv = jnp.concatenate(
        [sum(st_ref[1, j, pl.ds(HALO - 1 + k, t), :] * conv_w[j][k:k + 1] for k in range(3))
         for j in range(n_slabs)], axis=1)
    y_b = _dot(h_mid, w_in_cols(1)) * conv
    ga = _dot(h_mid, w_in_cols(4))
    gb = _dot(h_mid, w_in_cols(5))
    m = jax.nn.sigmoid(ga) * y_a + jax.nn.sigmoid(gb) * y_b
    m_ref[:, lo:hi] = m.astype(jnp.bfloat16)


def _mixer_kernel(x_ref, xan_ref, xnn_ref, w_in_ref, pool_w_ref, pool_scale_ref, conv_w_ref,
                  w_out_ref, g_pre_ref, g_post_ref, o_ref, ha_ref, hb_ref, ma_ref, mb_ref,
                  sta_ref, stb_ref, *, seq_len):
    t = ROW_TILE
    step = pl.program_id(0)
    pos_a0 = (step * 2 * t) & (seq_len - 1)
    g_pre = g_pre_ref[...]

    def norm(x):
        return _rmsnorm(x, g_pre).astype(jnp.bfloat16)

    @pl.when(step == 0)
    def _():
        ha_ref[0:HALO, :] = jnp.zeros((HALO, D_MODEL), jnp.bfloat16)
        ha_ref[HALO:HALO + t, :] = norm(x_ref[0:t, :])
        ha_ref[HALO + t:, :] = norm(x_ref[t:t + HALO, :])

    b_has_next = pos_a0 + 2 * t != seq_len
    hb_ref[0:HALO, :] = norm(x_ref[t - HALO:t, :])
    hb_ref[HALO:HALO + t, :] = norm(x_ref[t:2 * t, :])
    hb_ref[HALO + t:, :] = jnp.where(b_has_next, norm(xan_ref[0:HALO, :]), 0.0).astype(jnp.bfloat16)

    pos_a = pos_a0 + lax.broadcasted_iota(jnp.int32, (t, LANES_V7X), 0)
    pos_b = pos_a + t
    weights = (w_in_ref, pool_w_ref, pool_scale_ref, conv_w_ref)
    n_chunks = len(POOL_WINDOWS)
    for c in range(n_chunks):
        _mix_chunk(ha_ref, ma_ref, sta_ref, pos_a, c, seq_len, *weights)
        if c == n_chunks - 1:
            an_has_prev = ((pos_a0 + 2 * t) & (seq_len - 1)) != 0
            ha_ref[0:HALO, :] = jnp.where(
                an_has_prev, norm(x_ref[2 * t - HALO:2 * t, :]), 0.0).astype(jnp.bfloat16)
            ha_ref[HALO:HALO + t, :] = norm(xan_ref[...])
            ha_ref[HALO + t:, :] = norm(xnn_ref[...])
        _mix_chunk(hb_ref, mb_ref, stb_ref, pos_b, c, seq_len, *weights)

    g_post = g_post_ref[...]
    r_a = _dot(ma_ref[...], w_out_ref[...])
    r_b = _dot(mb_ref[...], w_out_ref[...])
    o_ref[0:t, :] = x_ref[0:t, :] + _rmsnorm(r_a, g_post)
    o_ref[t:2 * t, :] = x_ref[t:2 * t, :] + _rmsnorm(r_b, g_post)


def _ffn_kernel(x_ref, w_gate_ref, w_up_ref, w_down_ref, g_pre_ref, g_post_ref, o_ref,
                f_ref, hid_ref):
    t = ROW_TILE
    halves = (slice(0, t), slice(t, 2 * t))
    g_pre = g_pre_ref[...]
    for rows in halves:
        f_ref[rows, :] = _rmsnorm(x_ref[rows, :], g_pre).astype(jnp.bfloat16)
    for c in range(FFN_HIDDEN // COL_CHUNK):
        cols = slice(c * COL_CHUNK, (c + 1) * COL_CHUNK)
        for rows in halves:
            f = f_ref[rows, :]
            gate = _dot(f, w_gate_ref[:, cols])
            up = _dot(f, w_up_ref[:, cols])
            hid_ref[rows, cols] = (jax.nn.silu(gate) * up).astype(jnp.bfloat16)
    g_post = g_post_ref[...]
    for rows in halves:
        r = _dot(hid_ref[rows, :], w_down_ref[...])
        o_ref[rows, :] = x_ref[rows, :] + _rmsnorm(r, g_post)


def _layer_resident(stacked, layer):
    shape = stacked.shape[1:]
    return pl.BlockSpec((None,) + shape, lambda i: (layer,) + (0,) * len(shape),
                        pipeline_mode=pl.Buffered(1))


def _compiler_params():
    return pltpu.CompilerParams(dimension_semantics=("arbitrary",),
                                vmem_limit_bytes=VMEM_LIMIT_BYTES_V7X)


def _mixer(x, seq_len, layer, w_in, pool_w, pool_scale, conv_w, w_out, g_pre, g_post):
    rows = x.shape[0]
    t = ROW_TILE
    n_tiles = rows // t
    n_halo_blocks = rows // HALO
    step_spec = pl.BlockSpec((2 * t, D_MODEL), lambda i: (i, 0))
    next_a_spec = pl.BlockSpec((t, D_MODEL), lambda i: (jnp.minimum(2 * i + 2, n_tiles - 1), 0))
    next_halo_spec = pl.BlockSpec(
        (HALO, D_MODEL),
        lambda i: (jnp.minimum((2 * i + 3) * (t // HALO), n_halo_blocks - 1), 0))
    params = (w_in, pool_w, pool_scale, conv_w, w_out, g_pre, g_post)
    stage_shape = (3, COL_CHUNK // LANES_V7X, t + 2 * HALO, LANES_V7X)
    return pl.pallas_call(
        functools.partial(_mixer_kernel, seq_len=seq_len),
        grid=(rows // (2 * t),),
        in_specs=([step_spec, next_a_spec, next_halo_spec]
                  + [_layer_resident(p, layer) for p in params]),
        out_specs=step_spec,
        out_shape=jax.ShapeDtypeStruct(x.shape, x.dtype),
        scratch_shapes=[pltpu.VMEM((t + 2 * HALO, D_MODEL), jnp.bfloat16),
                        pltpu.VMEM((t + 2 * HALO, D_MODEL), jnp.bfloat16),
                        pltpu.VMEM((t, D_MODEL), jnp.bfloat16),
                        pltpu.VMEM((t, D_MODEL), jnp.bfloat16),
                        pltpu.VMEM(stage_shape, jnp.float32),
                        pltpu.VMEM(stage_shape, jnp.float32)],
        compiler_params=_compiler_params(),
        name="mixer",
    )(x, x, x, *params)


def _ffn(x, layer, w_gate, w_up, w_down, g_pre, g_post):
    rows = x.shape[0]
    t = ROW_TILE
    step_spec = pl.BlockSpec((2 * t, D_MODEL), lambda i: (i, 0))
    params = (w_gate, w_up, w_down, g_pre, g_post)
    return pl.pallas_call(
        _ffn_kernel,
        grid=(rows // (2 * t),),
        in_specs=[step_spec] + [_layer_resident(p, layer) for p in params],
        out_specs=step_spec,
        out_shape=jax.ShapeDtypeStruct(x.shape, x.dtype),
        scratch_shapes=[pltpu.VMEM((2 * t, D_MODEL), jnp.bfloat16),
                        pltpu.VMEM((2 * t, FFN_HIDDEN), jnp.bfloat16)],
        compiler_params=_compiler_params(),
        name="ffn",
    )(x, *params)


def kernel(x_prompt, x_sample, w_in, pool_w, pool_scale, conv_w, w_out, w_ffn_gate, w_ffn_up,
           w_ffn_down, g_pre_mix, g_post_mix, g_pre_ffn, g_post_ffn):
    depth = w_in.shape[0]
    bf16 = jnp.bfloat16
    w_in, pool_w, w_out = w_in.astype(bf16), pool_w.astype(bf16), w_out.astype(bf16)
    w_ffn_gate, w_ffn_up, w_ffn_down = (w_ffn_gate.astype(bf16), w_ffn_up.astype(bf16),
                                        w_ffn_down.astype(bf16))
    pool_scale, g_pre_mix, g_post_mix, g_pre_ffn, g_post_ffn = (
        p[:, None, :] for p in (pool_scale, g_pre_mix, g_post_mix, g_pre_ffn, g_post_ffn))
    outs = []
    for x in (x_prompt, x_sample):
        batch, seq_len, _ = x.shape
        assert seq_len & (seq_len - 1) == 0 and seq_len % (2 * ROW_TILE) == 0
        y = x.reshape(batch * seq_len, D_MODEL)
        for layer in range(depth):
            y = _mixer(y, seq_len, layer, w_in, pool_w, pool_scale, conv_w, w_out,
                       g_pre_mix, g_post_mix)
            y = _ffn(y, layer, w_ffn_gate, w_ffn_up, w_ffn_down, g_pre_ffn, g_post_ffn)
        outs.append(y.reshape(x.shape))
    return tuple(outs)
```
